```python
import math
import jax, jax.numpy as jnp
from jax import lax
import numpy as np

D_MODEL = 1024
BATCH = 8
SEQ = 2048
DEPTH = 1

CHUNK = 64

MIX_WIDTH = D_MODEL
POOL_WIDTH = MIX_WIDTH // 2
POOL_WINDOWS = (2, 4, 8, 16)
POOL_GROUPS = len(POOL_WINDOWS)
POOL_GROUP_DIM = POOL_WIDTH // POOL_GROUPS
SB_WIDTH = MIX_WIDTH - POOL_WIDTH
SB_HEADS = 8
SB_HEAD_DIM = SB_WIDTH // SB_HEADS
Q_BLOCK = 128
IN_PROJ_WIDTH = POOL_WIDTH + 3 * SB_WIDTH

PEER_HEADS = 8
PEER_N_KEYS = 128
PEER_N_EXPERTS = PEER_N_KEYS * PEER_N_KEYS
PEER_TOPK = 16
PEER_KEY_DIM = 256
PEER_HALF_DIM = PEER_KEY_DIM // 2
PEER_TOKEN_BLOCK = 128

LN_EPS = 1e-5
DN_ALPHA = (2.0 * DEPTH) ** 0.25
DN_BETA = (8.0 * DEPTH) ** -0.25

kernel_name = "hybrid_pool_stickbreak_peer_deepnorm"


def layer_norm(x, g, b):
    xf = x.astype(jnp.float32)
    mu = jnp.mean(xf, axis=-1, keepdims=True)
    var = jnp.mean(jnp.square(xf - mu), axis=-1, keepdims=True)
    y = (xf - mu) * lax.rsqrt(var + LN_EPS) * g.astype(jnp.float32) + b.astype(jnp.float32)
    return y.astype(x.dtype)


def pool_mixer(u, pool_w, pool_b, pool_scale):
    B, S, _ = u.shape
    ug = u.reshape(B, S, POOL_GROUPS, POOL_GROUP_DIM).astype(jnp.float32)
    cs = jnp.cumsum(ug, axis=1)
    pos = jnp.arange(S)
    outs = []
    for g, w in enumerate(POOL_WINDOWS):
        c = cs[:, :, g]
        shifted = jnp.pad(c, ((0, 0), (w, 0), (0, 0)))[:, :S]
        count = jnp.minimum(pos + 1, w).astype(jnp.float32)[None, :, None]
        outs.append((c - shifted) / count - ug[:, :, g])
    pooled = jnp.stack(outs, axis=2).astype(u.dtype)
    mixed = jnp.einsum('bsgc,gcd->bsgd', pooled, pool_w) + pool_b
    return mixed.reshape(B, S, POOL_WIDTH) * pool_scale


def stick_breaking_attention(q, k, v):
    S = q.shape[2]
    scale = 1.0 / math.sqrt(SB_HEAD_DIM)
    outs = []
    for i in range(S // Q_BLOCK):
        q0 = i * Q_BLOCK
        kend = q0 + Q_BLOCK
        z = jnp.einsum('bhqd,bhkd->bhqk', q[:, :, q0:kend], k[:, :, :kend]).astype(jnp.float32) * scale
        t_idx = q0 + jnp.arange(Q_BLOCK)[:, None]
        s_idx = jnp.arange(kend)[None, :]
        mask = s_idx < t_idx
        log_1m_beta = jnp.where(mask, jax.nn.log_sigmoid(-z), 0.0)
        after = lax.cumsum(log_1m_beta, axis=3, reverse=True) - log_1m_beta
        a = jnp.where(mask, jnp.exp(jax.nn.log_sigmoid(z) + after), 0.0)
        outs.append(jnp.einsum('bhqk,bhkd->bhqd', a.astype(v.dtype), v[:, :, :kend]))
    return jnp.concatenate(outs, axis=2)


def hybrid_mixer(x, w_in, pool_w, pool_b, pool_scale, w_out):
    B, S, _ = x.shape
    h = x @ w_in
    u = h[..., :POOL_WIDTH]
    qkv = h[..., POOL_WIDTH:].reshape(B, S, 3, SB_HEADS, SB_HEAD_DIM)
    q = jnp.transpose(qkv[:, :, 0], (0, 2, 1, 3))
    k = jnp.transpose(qkv[:, :, 1], (0, 2, 1, 3))
    v = jnp.transpose(qkv[:, :, 2], (0, 2, 1, 3))
    a_out = pool_mixer(u, pool_w, pool_b, pool_scale)
    b_out = jnp.transpose(stick_breaking_attention(q, k, v), (0, 2, 1, 3)).reshape(B, S, SB_WIDTH)
    return jnp.concatenate([a_out, b_out], axis=-1) @ w_out


def peer(x, w_query, sub_keys, u_table, v_table):
    B, S, D = x.shape
    T = B * S
    xt = x.reshape(T, D)
    q = (xt @ w_query).reshape(T, PEER_HEADS, 2, PEER_HALF_DIM)
    scores = jnp.einsum('thpc,pnc->thpn', q, sub_keys).astype(jnp.float32)
    top_s, top_i = lax.top_k(scores, PEER_TOPK)
    cand_s = (top_s[:, :, 0, :, None] + top_s[:, :, 1, None, :]).reshape(T, PEER_HEADS, PEER_TOPK * PEER_TOPK)
    cand_i = (top_i[:, :, 0, :, None] * PEER_N_KEYS + top_i[:, :, 1, None, :]).reshape(T, PEER_HEADS, PEER_TOPK * PEER_TOPK)
    best_s, best_pos = lax.top_k(cand_s, PEER_TOPK)
    expert_idx = jnp.take_along_axis(cand_i, best_pos, axis=-1)
    gate = jax.nn.softmax(best_s, axis=-1).astype(x.dtype)

    n_blocks = T // PEER_TOKEN_BLOCK

    def block_fn(args):
        xb, idx, g = args
        u_sel = u_table[idx]
        hid = jnp.einsum('td,thkd->thk', xb, u_sel)
        act = jax.nn.gelu(hid, approximate=False) * g
        v_sel = v_table[idx]
        return jnp.einsum('thk,thkd->td', act, v_sel)

    out = lax.map(block_fn, (xt.reshape(n_blocks, PEER_TOKEN_BLOCK, D),
                             expert_idx.reshape(n_blocks, PEER_TOKEN_BLOCK, PEER_HEADS, PEER_TOPK),
                             gate.reshape(n_blocks, PEER_TOKEN_BLOCK, PEER_HEADS, PEER_TOPK)))
    return out.reshape(B, S, D)


def setup_inputs(seed: int = 0) -> dict:
    key = jax.random.key(seed)
    ks = jax.random.split(key, 20)
    f32 = jnp.float32
    L = DEPTH
    D = D_MODEL

    def nrm(k, shape, scale):
        return jax.random.normal(k, shape, f32) * scale

    x = jax.random.normal(ks[0], (BATCH, SEQ, D), f32)
    ln0_g = 1.0 + nrm(ks[1], (D,), 0.02)
    ln0_b = nrm(ks[2], (D,), 0.02)
    w_in_pqk = nrm(ks[3], (L, D, POOL_WIDTH + 2 * SB_WIDTH), D ** -0.5)
    w_in_v = nrm(ks[4], (L, D, SB_WIDTH), D ** -0.5 * DN_BETA)
    w_in = jnp.concatenate([w_in_pqk, w_in_v], axis=-1)
    pool_w = nrm(ks[5], (L, POOL_GROUPS, POOL_GROUP_DIM, POOL_GROUP_DIM), POOL_GROUP_DIM ** -0.5)
    pool_b = nrm(ks[6], (L, POOL_GROUPS, POOL_GROUP_DIM), 0.02)
    pool_scale = 1.0 + nrm(ks[7], (L, POOL_WIDTH), 0.1)
    w_out = nrm(ks[8], (L, MIX_WIDTH, D), MIX_WIDTH ** -0.5 * DN_BETA)
    ln1_g = 1.0 + nrm(ks[9], (L, D), 0.02)
    ln1_b = nrm(ks[10], (L, D), 0.02)
    peer_wq = nrm(ks[11], (L, D, PEER_HEADS * PEER_KEY_DIM), D ** -0.5)
    peer_sub_keys = nrm(ks[12], (L, 2, PEER_N_KEYS, PEER_HALF_DIM), PEER_HALF_DIM ** -0.5)
    peer_u = nrm(ks[13], (L, PEER_N_EXPERTS, D), D ** -0.5)
    peer_v = nrm(ks[14], (L, PEER_N_EXPERTS, D), DN_BETA)
    ln2_g = 1.0 + nrm(ks[15], (L, D), 0.02)
    ln2_b = nrm(ks[16], (L, D), 0.02)
    return {"x": x, "ln0_g": ln0_g, "ln0_b": ln0_b, "w_in": w_in, "pool_w": pool_w,
            "pool_b": pool_b, "pool_scale": pool_scale, "w_out": w_out, "ln1_g": ln1_g,
            "ln1_b": ln1_b, "peer_wq": peer_wq, "peer_sub_keys": peer_sub_keys,
            "peer_u": peer_u, "peer_v": peer_v, "ln2_g": ln2_g, "ln2_b": ln2_b}


def reference(x, ln0_g, ln0_b, w_in, pool_w, pool_b, pool_scale, w_out, ln1_g, ln1_b,
              peer_wq, peer_sub_keys, peer_u, peer_v, ln2_g, ln2_b):
    h = layer_norm(x, ln0_g, ln0_b)
    for l in range(DEPTH):
        mix = hybrid_mixer(h, w_in[l], pool_w[l], pool_b[l], pool_scale[l], w_out[l])
        h = layer_norm(DN_ALPHA * h + mix, ln1_g[l], ln1_b[l])
        ffn = peer(h, peer_wq[l], peer_sub_keys[l], peer_u[l], peer_v[l])
        h = layer_norm(DN_ALPHA * h + ffn, ln2_g[l], ln2_b[l])
    return h
```

```python
import functools
import math

import jax
import jax.numpy as jnp
from jax import lax
from jax.experimental import pallas as pl
from jax.experimental.pallas import tpu as pltpu

F32 = jnp.float32
BF16 = jnp.bfloat16

LANES = 128
LN_EPS = 1e-5
DEPTH = 1
DN_ALPHA = (2.0 * DEPTH) ** 0.25

POOL_WINDOWS = (2, 4, 8, 16)
SB_HEAD_DIM = 64
Q_BLOCK = 128
PEER_HEADS = 8
PEER_N_KEYS = 128
PEER_TOPK = 16

VMEM_LIMIT_BYTES = 56 * 1024 * 1024


def _params(*semantics):
    return pltpu.CompilerParams(dimension_semantics=semantics,
                                vmem_limit_bytes=VMEM_LIMIT_BYTES)


def _layer_norm(x, g, b):
    mu = jnp.mean(x, axis=-1, keepdims=True)
    xc = x - mu
    var = jnp.mean(xc * xc, axis=-1, keepdims=True)
    return xc * lax.rsqrt(var + LN_EPS) * g + b


def _dot(a, b):
    return jnp.dot(a, b, preferred_element_type=F32)


def _dot_nt(a, b):
    return lax.dot_general(a, b, (((1,), (1,)), ((), ())), preferred_element_type=F32)


def _ln_inproj_kernel(x_ref, g_ref, b_ref, w_ref, h_ref, u_ref, q_ref, k_ref, v_ref):
    h = _layer_norm(x_ref[...], g_ref[...], b_ref[...])
    h_ref[...] = h
    hb = h.astype(BF16)
    width = u_ref.shape[1]
    u_ref[...] = _dot(hb, w_ref[:, 0:width])
    q_ref[...] = _dot(hb, w_ref[:, width:2 * width]).astype(BF16)
    k_ref[...] = _dot(hb, w_ref[:, 2 * width:3 * width]).astype(BF16)
    v_ref[...] = _dot(hb, w_ref[:, 3 * width:4 * width]).astype(BF16)


def _ln_inproj(x, g, b, w_in, tm):
    T, D = x.shape
    width = w_in.shape[1] // 4
    row = lambda i: (i, 0)
    fixed = lambda i: (0, 0)
    return pl.pallas_call(
        _ln_inproj_kernel,
        grid=(T // tm,),
        in_specs=[pl.BlockSpec((tm, D), row), pl.BlockSpec((1, D), fixed),
                  pl.BlockSpec((1, D), fixed), pl.BlockSpec(w_in.shape, fixed)],
        out_specs=[pl.BlockSpec((tm, D), row)] + [pl.BlockSpec((tm, width), row)] * 4,
        out_shape=[jax.ShapeDtypeStruct((T, D), F32), jax.ShapeDtypeStruct((T, width), F32)]
        + [jax.ShapeDtypeStruct((T, width), BF16)] * 3,
        compiler_params=_params("parallel"),
        name="ln_inproj",
    )(x, g, b, w_in)


def _pool_kernel(u_ref, w_ref, b_ref, s_ref, o_ref):
    S = u_ref.shape[0]
    t = lax.broadcasted_iota(jnp.int32, (S, LANES), 0)
    for g, window in enumerate(POOL_WINDOWS):
        cols = slice(g * LANES, (g + 1) * LANES)
        x = u_ref[:, cols]
        s = x
        k = 1
        while k < window:
            s = s + jnp.where(t >= k, pltpu.roll(s, k, axis=0), 0.0)
            k *= 2
        count = jnp.minimum(t + 1, window).astype(F32)
        pooled = s / count - x
        mixed = _dot(pooled.astype(BF16), w_ref[g]) + b_ref[g:g + 1, :]
        o_ref[:, cols] = (mixed * s_ref[:, cols]).astype(BF16)


def _pool(u, pool_w, pool_b, pool_scale, S):
    T, W = u.shape
    return pl.pallas_call(
        _pool_kernel,
        grid=(T // S,),
        in_specs=[pl.BlockSpec((S, W), lambda i: (i, 0)),
                  pl.BlockSpec(pool_w.shape, lambda i: (0, 0, 0)),
                  pl.BlockSpec(pool_b.shape, lambda i: (0, 0)),
                  pl.BlockSpec(pool_scale.shape, lambda i: (0, 0))],
        out_specs=pl.BlockSpec((S, W), lambda i: (i, 0)),
        out_shape=jax.ShapeDtypeStruct((T, W), BF16),
        compiler_params=_params("parallel"),
        name="pool",
    )(u, pool_w, pool_b, pool_scale)


def _attn_kernel(q_ref, k_ref, v_ref, mstrict_ref, mones_ref, o_ref):
    qi = pl.program_id(2)
    scale = 1.0 / math.sqrt(SB_HEAD_DIM)
    lane = lax.broadcasted_iota(jnp.int32, (1, 2 * SB_HEAD_DIM), 1)
    first = lane < SB_HEAD_DIM
    q = q_ref[...]
    zero = jnp.zeros_like(q)
    q_a = jnp.where(first, q, zero)
    q_b = jnp.where(first, zero, q)
    m_strict = mstrict_ref[...]
    m_ones = mones_ref[...]
    t_idx = lax.broadcasted_iota(jnp.int32, (Q_BLOCK, 2 * Q_BLOCK), 0)
    s_idx = lax.broadcasted_iota(jnp.int32, (Q_BLOCK, 2 * Q_BLOCK), 1) % Q_BLOCK
    causal = s_idx < t_idx

    def block(kb, carry, acc, diagonal):
        start = pl.multiple_of(kb * Q_BLOCK, Q_BLOCK)
        kblk = k_ref[pl.ds(start, Q_BLOCK), :]
        vblk = v_ref[pl.ds(start, Q_BLOCK), :]
        z = jnp.concatenate([_dot_nt(q_a, kblk), _dot_nt(q_b, kblk)], axis=1) * scale
        sp = jnp.maximum(z, 0.0) + jnp.log1p(jnp.exp(-jnp.abs(z)))
        neg = -sp
        if diagonal:
            neg = jnp.where(causal, neg, 0.0)
        hi = neg.astype(BF16)
        lo = (neg - hi.astype(F32)).astype(BF16)
        after = _dot(hi, m_strict) + _dot(lo, m_strict) + carry
        rows = _dot(hi, m_ones) + _dot(lo, m_ones)
        a = jnp.exp(z - sp + after)
        if diagonal:
            a = jnp.where(causal, a, 0.0)
        a = a.astype(BF16)
        vzero = jnp.zeros_like(vblk)
        v_a = jnp.where(first, vblk, vzero)
        v_b = jnp.where(first, vzero, vblk)
        acc = acc + _dot(a[:, :Q_BLOCK], v_a) + _dot(a[:, Q_BLOCK:], v_b)
        return carry + rows, acc

    carry0 = jnp.zeros((Q_BLOCK, 2 * Q_BLOCK), F32)
    acc0 = jnp.zeros((Q_BLOCK, 2 * SB_HEAD_DIM), F32)
    carry, acc = block(qi, carry0, acc0, True)

    def body(i, state):
        return block(qi - 1 - i, state[0], state[1], False)

    _, acc = lax.fori_loop(0, qi, body, (carry, acc))
    o_ref[...] = acc.astype(o_ref.dtype)


def _sb_attn(q, k, v, S):
    T, W = q.shape
    nb = S // Q_BLOCK
    pair = 2 * SB_HEAD_DIM
    r = jnp.arange(2 * Q_BLOCK)
    same = (r[:, None] // Q_BLOCK) == (r[None, :] // Q_BLOCK)
    m_strict = (same & (r[:, None] > r[None, :])).astype(BF16)
    m_ones = same.astype(BF16)
    qmap = lambda b, p, i: (b * nb + i, p)
    kmap = lambda b, p, i: (b, p)
    cmap = lambda b, p, i: (0, 0)
    return pl.pallas_call(
        _attn_kernel,
        grid=(T // S, W // pair, nb),
        in_specs=[pl.BlockSpec((Q_BLOCK, pair), qmap), pl.BlockSpec((S, pair), kmap),
                  pl.BlockSpec((S, pair), kmap), pl.BlockSpec(m_strict.shape, cmap),
                  pl.BlockSpec(m_ones.shape, cmap)],
        out_specs=pl.BlockSpec((Q_BLOCK, pair), qmap),
        out_shape=jax.ShapeDtypeStruct((T, W), BF16),
        compiler_params=_params("parallel", "parallel", "arbitrary"),
        name="sb_attn",
    )(q, k, v, m_strict, m_ones)


def _outproj_kernel(a_ref, b_ref, h0_ref, wout_ref, g_ref, bb_ref, wq_ref, sk_ref,
                    h1_ref, x1_ref, st_ref):
    half = a_ref.shape[1]
    mix = _dot(a_ref[...], wout_ref[0:half, :]) + _dot(b_ref[...], wout_ref[half:2 * half, :])
    h1 = _layer_norm(DN_ALPHA * h0_ref[...] + mix, g_ref[...], bb_ref[...])
    h1_ref[...] = h1
    xb = h1.astype(BF16)
    x1_ref[...] = xb
    for hp in range(st_ref.shape[0]):
        qp = _dot(xb, wq_ref[:, hp * LANES:(hp + 1) * LANES]).astype(BF16)
        st_ref[hp] = _dot_nt(sk_ref[hp % 2], qp)


def _outproj(a_out, b_out, h0, w_out, g, b, w_q, sub_keys, tm):
    T, D = h0.shape
    half = a_out.shape[1]
    n_hp = w_q.shape[1] // LANES
    row = lambda i: (i, 0)
    fixed = lambda i: (0, 0)
    return pl.pallas_call(
        _outproj_kernel,
        grid=(T // tm,),
        in_specs=[pl.BlockSpec((tm, half), row), pl.BlockSpec((tm, half), row),
                  pl.BlockSpec((tm, D), row), pl.BlockSpec(w_out.shape, fixed),
                  pl.BlockSpec((1, D), fixed), pl.BlockSpec((1, D), fixed),
                  pl.BlockSpec(w_q.shape, fixed),
                  pl.BlockSpec(sub_keys.shape, lambda i: (0, 0, 0))],
        out_specs=[pl.BlockSpec((tm, D), row), pl.BlockSpec((tm, D), row),
                   pl.BlockSpec((n_hp, PEER_N_KEYS, tm), lambda i: (0, 0, i))],
        out_shape=[jax.ShapeDtypeStruct((T, D), F32), jax.ShapeDtypeStruct((T, D), BF16),
                   jax.ShapeDtypeStruct((n_hp, PEER_N_KEYS, T), F32)],
        compiler_params=_params("parallel"),
        name="outproj",
    )(a_out, b_out, h0, w_out, g, b, w_q, sub_keys)


def _candidate_pairs():
    return [(ka, kb) for ka in range(PEER_TOPK) for kb in range(PEER_TOPK)
            if (ka + 1) * (kb + 1) <= PEER_TOPK]


def _top16(s, key_iota):
    rank = jnp.full(s.shape, float(PEER_TOPK), F32)
    vals = []
    for r in range(PEER_TOPK):
        m = jnp.max(s, axis=0, keepdims=True)
        first = jnp.min(jnp.where(s == m, key_iota, float(PEER_N_KEYS)), axis=0, keepdims=True)
        hit = key_iota == first
        rank = jnp.where(hit, float(r), rank)
        s = jnp.where(hit, -jnp.inf, s)
        vals.append(m)
    return jnp.concatenate(vals, axis=0), rank


def _topk_kernel(st_ref, rankb_ref, eb_ref, cnt_ref, c_ref):
    n_heads = rankb_ref.shape[0]
    tk = st_ref.shape[2]
    key_iota = lax.broadcasted_iota(jnp.int32, (PEER_N_KEYS, tk), 0).astype(F32)
    pairs = _candidate_pairs()
    vals_a, vals_b, ranks_a = [], [], []
    for h in range(n_heads):
        va, ra = _top16(st_ref[2 * h], key_iota)
        vb, rb = _top16(st_ref[2 * h + 1], key_iota)
        rankb_ref[h] = rb
        eb_ref[h] = jnp.exp(st_ref[2 * h + 1] - vb[0:1, :])
        vals_a.append(va)
        vals_b.append(vb)
        ranks_a.append(ra)
    A = [jnp.concatenate([vals_a[h][r:r + 1, :] for h in range(n_heads)], axis=0)
         for r in range(PEER_TOPK)]
    B = [jnp.concatenate([vals_b[h][r:r + 1, :] for h in range(n_heads)], axis=0)
         for r in range(PEER_TOPK)]
    cand = {p: A[p[0]] + B[p[1]] for p in pairs}
    zero = jnp.zeros_like(A[0])
    cnt = [zero] * PEER_TOPK
    zsum = zero
    for (ka, kb) in pairs:
        c = cand[(ka, kb)]
        ahead = zero + float((ka + 1) * (kb + 1) - 1)
        for (la, lb) in pairs:
            if (la <= ka and lb <= kb) or (la >= ka and lb >= kb):
                continue
            o = cand[(la, lb)]
            before = (o >= c) if (la * PEER_TOPK + lb) < (ka * PEER_TOPK + kb) else (o > c)
            ahead = ahead + jnp.where(before, 1.0, 0.0)
        chosen = ahead < float(PEER_TOPK)
        cnt[ka] = cnt[ka] + jnp.where(chosen, 1.0, 0.0)
        e = jnp.exp(A[ka] - A[0]) * jnp.exp(B[kb] - B[0])
        zsum = zsum + jnp.where(chosen, e, 0.0)
    inv_z = 1.0 / zsum
    for h in range(n_heads):
        ra = ranks_a[h]
        cnt_i = jnp.zeros_like(ra)
        for ka in range(PEER_TOPK):
            cnt_i = jnp.where(ra == float(ka), cnt[ka][h:h + 1, :], cnt_i)
        cnt_ref[h] = cnt_i
        c_ref[h] = jnp.exp(st_ref[2 * h] - vals_a[h][0:1, :]) * inv_z[h:h + 1, :]


def _topk(scores_t, tk):
    n_hp, n_keys, T = scores_t.shape
    n_heads = n_hp // 2
    out = jax.ShapeDtypeStruct((n_heads, n_keys, T), F32)
    spec = pl.BlockSpec((n_heads, n_keys, tk), lambda i: (0, 0, i))
    return pl.pallas_call(
        _topk_kernel,
        grid=(T // tk,),
        in_specs=[pl.BlockSpec((n_hp, n_keys, tk), lambda i: (0, 0, i))],
        out_specs=[spec] * 4,
        out_shape=[out] * 4,
        compiler_params=_params("parallel"),
        name="topk",
    )(scores_t)


def _peer_kernel(x_ref, u_ref, vt_ref, rankb_ref, eb_ref, cnt_ref, c_ref, h1_ref, g_ref, b_ref,
                 o_ref, acc_ref, act_ref):
    ci = pl.program_id(1)
    n_heads = rankb_ref.shape[0]
    rows_per_step = cnt_ref.shape[1]

    @pl.when(ci == 0)
    def _():
        acc_ref[...] = jnp.zeros_like(acc_ref)

    hid_t = _dot_nt(u_ref[...], x_ref[...])
    for il in range(rows_per_step):
        gate = jnp.zeros((PEER_N_KEYS, hid_t.shape[1]), F32)
        for h in range(n_heads):
            keep = rankb_ref[h] < cnt_ref[h, il:il + 1, :]
            gate = gate + jnp.where(keep, eb_ref[h], 0.0) * c_ref[h, il:il + 1, :]
        rows = slice(il * PEER_N_KEYS, (il + 1) * PEER_N_KEYS)
        hid = hid_t[rows, :]
        act = 0.5 * hid * (1.0 + lax.erf(hid * (1.0 / math.sqrt(2.0))))
        act_ref[rows, :] = (act * gate).astype(BF16)
    acc_ref[...] += _dot(vt_ref[...], act_ref[...])

    @pl.when(ci == pl.num_programs(1) - 1)
    def _():
        y = DN_ALPHA * h1_ref[...] + acc_ref[...].T
        o_ref[...] = _layer_norm(y, g_ref[...], b_ref[...])


def _peer(x1, u_tab, v_tab_t, rank_b, e_b, cnt_i, c_i, h1, g, b, tm, te):
    T, D = h1.shape
    n_exp = u_tab.shape[0]
    n_heads = rank_b.shape[0]
    ni = te // PEER_N_KEYS
    tok = lambda t, c: (t, 0)
    fixed = lambda t, c: (0, 0)
    per_tok = lambda t, c: (0, 0, t)
    return pl.pallas_call(
        _peer_kernel,
        grid=(T // tm, n_exp // te),
        in_specs=[pl.BlockSpec((tm, D), tok),
                  pl.BlockSpec((te, D), lambda t, c: (c, 0)),
                  pl.BlockSpec((D, te), lambda t, c: (0, c)),
                  pl.BlockSpec((n_heads, PEER_N_KEYS, tm), per_tok),
                  pl.BlockSpec((n_heads, PEER_N_KEYS, tm), per_tok),
                  pl.BlockSpec((n_heads, ni, tm), lambda t, c: (0, c, t)),
                  pl.BlockSpec((n_heads, ni, tm), lambda t, c: (0, c, t)),
                  pl.BlockSpec((tm, D), tok),
                  pl.BlockSpec((1, D), fixed), pl.BlockSpec((1, D), fixed)],
        out_specs=pl.BlockSpec((tm, D), tok),
        out_shape=jax.ShapeDtypeStruct((T, D), F32),
        scratch_shapes=[pltpu.VMEM((D, tm), F32), pltpu.VMEM((te, tm), BF16)],
        compiler_params=_params("parallel", "arbitrary"),
        name="peer",
    )(x1, u_tab, v_tab_t, rank_b, e_b, cnt_i, c_i, h1, g, b)


def _layer(x, ln_in_g, ln_in_b, w_in, pool_w, pool_b, pool_scale, w_out, ln1_g, ln1_b,
           peer_wq, peer_sub_keys, peer_u, peer_v, ln2_g, ln2_b, S):
    T, D = x.shape
    row = lambda a: a.reshape(1, -1)
    h0, u, q, k, v = _ln_inproj(x, row(ln_in_g), row(ln_in_b), w_in.astype(BF16), tm=min(512, T))
    a_out = _pool(u, pool_w.astype(BF16), pool_b, row(pool_scale), S)
    b_out = _sb_attn(q, k, v, S)
    h1, x1, scores_t = _outproj(a_out, b_out, h0, w_out.astype(BF16), row(ln1_g), row(ln1_b),
                                peer_wq.astype(BF16), peer_sub_keys.astype(BF16), tm=min(512, T))
    rank_b, e_b, cnt_i, c_i = _topk(scores_t, tk=min(256, T))
    return _peer(x1, peer_u.astype(BF16), peer_v.astype(BF16).T, rank_b, e_b, cnt_i, c_i, h1,
                 row(ln2_g), row(ln2_b), tm=min(512, T), te=1024)


def kernel(x, ln0_g, ln0_b, w_in, pool_w, pool_b, pool_scale, w_out, ln1_g, ln1_b,
           peer_wq, peer_sub_keys, peer_u, peer_v, ln2_g, ln2_b):
    B, S, D = x.shape
    assert w_in.shape[0] == DEPTH
    h = _layer(x.reshape(B * S, D), ln0_g, ln0_b, w_in[0], pool_w[0], pool_b[0], pool_scale[0],
               w_out[0], ln1_g[0], ln1_b[0], peer_wq[0], peer_sub_keys[0], peer_u[0], peer_v[0],
               ln2_g[0], ln2_b[0], S)
    return h.reshape(B, S, D)
```
